```python
import math
import jax, jax.numpy as jnp
from jax import lax
import numpy as np

D_MODEL = 1024
BATCH = 2
SEQ = 8192
DEPTH = 4
DEC_BATCH = 32
DEC_SEQ = 4
PAST_LEN = 8192
PAGE_SIZE = 128

N_MIXERS = 2
N_REC = (DEPTH + 1) // 2
N_ATT = DEPTH // 2
D_FF = 2816
D_RNN = D_MODEL
N_LRU_BLOCKS = 8
LRU_BLOCK = D_RNN // N_LRU_BLOCKS
CONV_W = 4
LRU_C = 8.0
N_HEADS = 16
HEAD_DIM = D_MODEL // N_HEADS
WINDOWS = (128, 512, 2048)
DILATIONS = (1, 4, 16)
N_GROUPS = 3
BAND_BLOCK = 128
ROPE_THETA = 10000.0
EPS = 1e-6

kernel_name = "hawk_longnet_macaron_step"


def rms_norm(x, g):
    xf = x.astype(jnp.float32)
    y = xf * lax.rsqrt(jnp.mean(xf * xf, axis=-1, keepdims=True) + EPS)
    return (y * g.astype(jnp.float32)).astype(x.dtype)


def half_swiglu(x, g, w_up, w_down):
    gate, up = jnp.split(rms_norm(x, g) @ w_up, 2, axis=-1)
    return x + 0.5 * ((jax.nn.silu(gate) * up) @ w_down)


def rope(x, pos):
    half = HEAD_DIM // 2
    inv = jnp.power(ROPE_THETA, -jnp.arange(half, dtype=jnp.float32) / half)
    ang = pos.astype(jnp.float32)[:, None] * inv[None, :]
    cos, sin = jnp.cos(ang)[:, None, :], jnp.sin(ang)[:, None, :]
    xf = x.astype(jnp.float32)
    x1, x2 = xf[..., :half], xf[..., half:]
    return jnp.concatenate([x1 * cos - x2 * sin, x2 * cos + x1 * sin], axis=-1).astype(x.dtype)


def causal_conv(xb, conv_state, w, b):
    T = xb.shape[1]
    xp = jnp.concatenate([conv_state.astype(xb.dtype), xb], axis=1)
    y = b + xp[:, 0:T] * w[0]
    for k in range(1, CONV_W):
        y = y + xp[:, k:k + T] * w[k]
    return y, xp[:, -(CONV_W - 1):]


def rg_lru(x, h0, w_a, b_a, w_x, b_x, lam):
    N, T, _ = x.shape
    xf = x.astype(jnp.float32)
    xb = xf.reshape(N, T, N_LRU_BLOCKS, LRU_BLOCK)
    gate_a = jax.nn.sigmoid(jnp.einsum("ntbi,bij->ntbj", xb, w_a.astype(jnp.float32)).reshape(N, T, D_RNN) + b_a.astype(jnp.float32))
    gate_x = jax.nn.sigmoid(jnp.einsum("ntbi,bij->ntbj", xb, w_x.astype(jnp.float32)).reshape(N, T, D_RNN) + b_x.astype(jnp.float32))
    log_a = -LRU_C * gate_a * jax.nn.softplus(-lam.astype(jnp.float32))
    a = jnp.exp(log_a)
    bterm = jnp.sqrt(-jnp.expm1(2.0 * log_a)) * gate_x * xf
    bterm = bterm.at[:, 0].add(a[:, 0] * h0.astype(jnp.float32))

    def comb(l, r):
        return l[0] * r[0], r[0] * l[1] + r[1]

    _, h = lax.associative_scan(comb, (a, bterm), axis=1)
    return h, h[:, -1]


def recurrent_mixer(x, h0, conv0, w_in, b_in, w_conv, b_conv, w_a, b_a, w_x, b_x, lam, w_out, b_out):
    y_branch, x_branch = jnp.split(x @ w_in + b_in, 2, axis=-1)
    y_branch = jax.nn.gelu(y_branch, approximate=True)
    xc, conv_new = causal_conv(x_branch, conv0, w_conv, b_conv)
    h, h_last = rg_lru(xc, h0, w_a, b_a, w_x, b_x, lam)
    out = (h.astype(x.dtype) * y_branch) @ w_out + b_out
    return out, h_last, conv_new


def qkv_groups(x, w_qkv, pos):
    N, T, _ = x.shape
    qkv = (x @ w_qkv).reshape(N, T, N_GROUPS, 3, N_HEADS, HEAD_DIM)
    return [(rope(qkv[:, :, g, 0], pos), rope(qkv[:, :, g, 1], pos), qkv[:, :, g, 2]) for g in range(N_GROUPS)]


def banded_attn(q, k, v, n_back):
    N, L, H, hd = q.shape
    nb = -(-L // BAND_BLOCK)
    pad = nb * BAND_BLOCK - L
    padf = lambda t: jnp.pad(t, ((0, 0), (0, pad), (0, 0), (0, 0))).reshape(N, nb, BAND_BLOCK, H, hd)
    qb, kb, vb = padf(q), padf(k), padf(v)
    with_prev = lambda t: jnp.concatenate([jnp.pad(t[:, :-1], ((0, 0), (1, 0), (0, 0), (0, 0), (0, 0))), t], axis=2)
    kk, vv = with_prev(kb), with_prev(vb)
    s = jnp.einsum("nbqhd,nbkhd->nbhqk", qb.astype(jnp.float32), kk.astype(jnp.float32)) * (hd ** -0.5)
    iq = jnp.arange(BAND_BLOCK)
    ik = jnp.arange(2 * BAND_BLOCK)
    dist = (BAND_BLOCK + iq)[:, None] - ik[None, :]
    band = (dist >= 0) & (dist <= n_back)
    kpos_ok = (jnp.arange(nb)[:, None] * BAND_BLOCK - BAND_BLOCK + ik[None, :]) >= 0
    mask = band[None, :, :] & kpos_ok[:, None, :]
    s = jnp.where(mask[None, :, None], s, -jnp.inf)
    lse = jax.nn.logsumexp(s, axis=-1)
    p = jnp.exp(s - lse[..., None])
    o = jnp.einsum("nbhqk,nbkhd->nbqhd", p, vv.astype(jnp.float32)).reshape(N, nb * BAND_BLOCK, H, hd)[:, :L]
    lse = lse.transpose(0, 1, 3, 2).reshape(N, nb * BAND_BLOCK, H)[:, :L]
    return o, lse


def dilated_prompt_attn(q, k, v, window, dil):
    B, S, H, hd = q.shape
    L = S // dil
    to_streams = lambda t: t.reshape(B, L, dil, H, hd).transpose(0, 2, 1, 3, 4).reshape(B * dil, L, H, hd)
    o, lse = banded_attn(to_streams(q), to_streams(k), to_streams(v), window // dil)
    o = o.reshape(B, dil, L, H, hd).transpose(0, 2, 1, 3, 4).reshape(B, S, H, hd)
    lse = lse.reshape(B, dil, L, H).transpose(0, 2, 1, 3).reshape(B, S, H)
    return o, lse


def dilated_sample_attn(q, k, v, cache, window, dil):
    Nd, T, H, hd = q.shape
    C = cache.shape[1]
    k_all = jnp.concatenate([cache[:, :, 0].astype(k.dtype), k], axis=1)
    v_all = jnp.concatenate([cache[:, :, 1].astype(v.dtype), v], axis=1)
    n_keys = window // dil + 1
    idx = C + jnp.arange(T)[:, None] - dil * jnp.arange(n_keys)[None, :]
    valid = idx >= 0
    idx = jnp.maximum(idx, 0)
    kg = k_all[:, idx].astype(jnp.float32)
    vg = v_all[:, idx].astype(jnp.float32)
    s = jnp.einsum("bthd,btkhd->bthk", q.astype(jnp.float32), kg) * (hd ** -0.5)
    s = jnp.where(valid[None, :, None, :], s, -jnp.inf)
    lse = jax.nn.logsumexp(s, axis=-1)
    p = jnp.exp(s - lse[..., None])
    return jnp.einsum("bthk,btkhd->bthd", p, vg), lse


def merge_groups(outs, lses, w_o, dtype):
    alpha = jax.nn.softmax(jnp.stack(lses, 0), axis=0)
    o = jnp.einsum("gnth,gnthd->nthd", alpha, jnp.stack(outs, 0))
    N, T = o.shape[:2]
    return o.reshape(N, T, N_HEADS * HEAD_DIM).astype(dtype) @ w_o


def attn_prompt(x, w_qkv, w_o):
    B, S, _ = x.shape
    groups = qkv_groups(x, w_qkv, jnp.arange(S))
    outs, lses, rows = [], [], []
    for g, (q, k, v) in enumerate(groups):
        o, lse = dilated_prompt_attn(q, k, v, WINDOWS[g], DILATIONS[g])
        outs.append(o)
        lses.append(lse)
        c = min(WINDOWS[g], S)
        rows.append(jnp.stack([k[:, S - c:], v[:, S - c:]], axis=2))
    return merge_groups(outs, lses, w_o, x.dtype), rows


def attn_sample(x, caches, w_qkv, w_o):
    Nd, T, _ = x.shape
    groups = qkv_groups(x, w_qkv, PAST_LEN + jnp.arange(T))
    outs, lses, rows = [], [], []
    for g, (q, k, v) in enumerate(groups):
        o, lse = dilated_sample_attn(q, k, v, caches[g], WINDOWS[g], DILATIONS[g])
        outs.append(o)
        lses.append(lse)
        rows.append(jnp.stack([k, v], axis=2))
    return merge_groups(outs, lses, w_o, x.dtype), rows


def setup_inputs(seed: int = 0) -> dict:
    key = jax.random.key(seed)
    ks = iter(jax.random.split(key, 40))
    nrm = lambda shape, scale: scale * jax.random.normal(next(ks), shape, jnp.float32)
    gain = lambda shape: 1.0 + 0.02 * jax.random.normal(next(ks), shape, jnp.float32)
    u = jax.random.uniform(next(ks), (N_REC, D_RNN), jnp.float32, minval=0.9, maxval=0.999)
    s = u ** (1.0 / LRU_C)
    lru_lambda = jnp.log(s) - jnp.log1p(-s)
    inp = {
        "x_prompt": nrm((BATCH, SEQ, D_MODEL), 1.0),
        "x_sample": nrm((DEC_BATCH, DEC_SEQ, D_MODEL), 1.0),
        "state_h": nrm((N_REC, DEC_BATCH, D_RNN), 1.0),
        "state_conv": nrm((N_REC, DEC_BATCH, CONV_W - 1, D_RNN), 1.0),
    }
    for w in WINDOWS:
        inp["cache_kv_w%d" % w] = nrm((N_ATT, DEC_BATCH, min(w, PAST_LEN), 2, N_HEADS, HEAD_DIM), 1.0)
    inp.update({
        "norm_ffn1": gain((DEPTH, D_MODEL)),
        "norm_mix": gain((DEPTH, D_MODEL)),
        "norm_ffn2": gain((DEPTH, D_MODEL)),
        "w_ffn1_up": nrm((DEPTH, D_MODEL, 2 * D_FF), D_MODEL ** -0.5),
        "w_ffn1_down": nrm((DEPTH, D_FF, D_MODEL), D_FF ** -0.5),
        "w_ffn2_up": nrm((DEPTH, D_MODEL, 2 * D_FF), D_MODEL ** -0.5),
        "w_ffn2_down": nrm((DEPTH, D_FF, D_MODEL), D_FF ** -0.5),
        "w_rec_in": nrm((N_REC, D_MODEL, 2 * D_RNN), D_MODEL ** -0.5),
        "b_rec_in": nrm((N_REC, 2 * D_RNN), 0.02),
        "w_conv": nrm((N_REC, CONV_W, D_RNN), CONV_W ** -0.5),
        "b_conv": nrm((N_REC, D_RNN), 0.02),
        "w_gate_a": nrm((N_REC, N_LRU_BLOCKS, LRU_BLOCK, LRU_BLOCK), LRU_BLOCK ** -0.5),
        "b_gate_a": nrm((N_REC, D_RNN), 0.02),
        "w_gate_x": nrm((N_REC, N_LRU_BLOCKS, LRU_BLOCK, LRU_BLOCK), LRU_BLOCK ** -0.5),
        "b_gate_x": nrm((N_REC, D_RNN), 0.02),
        "lru_lambda": lru_lambda,
        "w_rec_out": nrm((N_REC, D_RNN, D_MODEL), D_RNN ** -0.5),
        "b_rec_out": nrm((N_REC, D_MODEL), 0.02),
        "w_qkv": nrm((N_ATT, D_MODEL, N_GROUPS * 3 * N_HEADS * HEAD_DIM), D_MODEL ** -0.5),
        "w_attn_out": nrm((N_ATT, N_HEADS * HEAD_DIM, D_MODEL), (N_HEADS * HEAD_DIM) ** -0.5),
        "norm_final": gain((D_MODEL,)),
    })
    return inp


def reference(x_prompt, x_sample, state_h, state_conv, cache_kv_w128, cache_kv_w512, cache_kv_w2048,
              norm_ffn1, norm_mix, norm_ffn2, w_ffn1_up, w_ffn1_down, w_ffn2_up, w_ffn2_down,
              w_rec_in, b_rec_in, w_conv, b_conv, w_gate_a, b_gate_a, w_gate_x, b_gate_x, lru_lambda,
              w_rec_out, b_rec_out, w_qkv, w_attn_out, norm_final):
    caches = (cache_kv_w128, cache_kv_w512, cache_kv_w2048)
    xp, xs = x_prompt, x_sample
    B = xp.shape[0]
    h_p, h_s, conv_p, conv_s = [], [], [], []
    kv_p = [[] for _ in range(N_GROUPS)]
    kv_s = [[] for _ in range(N_GROUPS)]
    for i in range(DEPTH):
        xp = half_swiglu(xp, norm_ffn1[i], w_ffn1_up[i], w_ffn1_down[i])
        xs = half_swiglu(xs, norm_ffn1[i], w_ffn1_up[i], w_ffn1_down[i])
        hp, hs = rms_norm(xp, norm_mix[i]), rms_norm(xs, norm_mix[i])
        j = i // N_MIXERS
        if i % N_MIXERS == 0:
            prm = (w_rec_in[j], b_rec_in[j], w_conv[j], b_conv[j], w_gate_a[j], b_gate_a[j],
                   w_gate_x[j], b_gate_x[j], lru_lambda[j], w_rec_out[j], b_rec_out[j])
            yp, hl_p, cv_p = recurrent_mixer(hp, jnp.zeros((B, D_RNN), jnp.float32),
                                             jnp.zeros((B, CONV_W - 1, D_RNN), hp.dtype), *prm)
            ys, hl_s, cv_s = recurrent_mixer(hs, state_h[j], state_conv[j], *prm)
            h_p.append(hl_p)
            h_s.append(hl_s)
            conv_p.append(cv_p)
            conv_s.append(cv_s)
        else:
            yp, rows_p = attn_prompt(hp, w_qkv[j], w_attn_out[j])
            ys, rows_s = attn_sample(hs, tuple(c[j] for c in caches), w_qkv[j], w_attn_out[j])
            for g in range(N_GROUPS):
                kv_p[g].append(rows_p[g])
                kv_s[g].append(rows_s[g])
        xp, xs = xp + yp, xs + ys
        xp = half_swiglu(xp, norm_ffn2[i], w_ffn2_up[i], w_ffn2_down[i])
        xs = half_swiglu(xs, norm_ffn2[i], w_ffn2_up[i], w_ffn2_down[i])
    y_prompt = rms_norm(xp, norm_final)
    y_sample = rms_norm(xs, norm_final)
    return (y_prompt, y_sample,
            jnp.stack(h_p, 0), jnp.stack(h_s, 0), jnp.stack(conv_p, 0), jnp.stack(conv_s, 0),
            jnp.stack(kv_p[0], 0), jnp.stack(kv_s[0], 0),
            jnp.stack(kv_p[1], 0), jnp.stack(kv_s[1], 0),
            jnp.stack(kv_p[2], 0), jnp.stack(kv_s[2], 0))
```

```python
import functools
import math

import jax
import jax.numpy as jnp
from jax import lax
from jax.experimental import pallas as pl
from jax.experimental.pallas import tpu as pltpu

D_MODEL = 1024
D_FF = 2816
D_RNN = 1024
N_LRU_BLOCKS = 8
LRU_BLOCK = D_RNN // N_LRU_BLOCKS
CONV_W = 4
LRU_C = 8.0
N_HEADS = 16
HEAD_DIM = 64
N_GROUPS = 3
WINDOWS = (128, 512, 2048)
DILATIONS = (1, 4, 16)
BAND = 128
ROPE_THETA = 10000.0
EPS = 1e-6
PAST_LEN = 8192

LANES = 128
SUBLANES = 8
VMEM_CAP = 60 * 1024 * 1024
NEG = -1e30

F32 = jnp.float32
BF16 = jnp.bfloat16


def _params(sem, vmem_bytes):
    return pltpu.CompilerParams(dimension_semantics=sem,
                                vmem_limit_bytes=int(min(vmem_bytes, VMEM_CAP)))


def _rms(x, g):
    return x * lax.rsqrt(jnp.mean(x * x, axis=-1, keepdims=True) + EPS) * g


def _dot(a, b):
    return jnp.dot(a, b, preferred_element_type=F32)


def _dot_split(a, b):
    hi = a.astype(BF16)
    lo = (a - hi.astype(F32)).astype(BF16)
    return _dot(hi, b) + _dot(lo, b)


def _ffn_body(x_ref, g_ref, wg_ref, wu_ref, wd_ref, gf_ref, o_ref, xn_ref, acc_ref, *, nf, final):
    f = pl.program_id(1)

    @pl.when(f == 0)
    def _():
        xn_ref[...] = _rms(x_ref[...], g_ref[...]).astype(BF16)
        acc_ref[...] = jnp.zeros_like(acc_ref)

    xn = xn_ref[...]
    gate = _dot(xn, wg_ref[...])
    up = _dot(xn, wu_ref[...])
    h = (gate * jax.nn.sigmoid(gate)) * up
    acc_ref[...] += _dot(h.astype(BF16), wd_ref[...])

    @pl.when(f == nf - 1)
    def _():
        y = x_ref[...] + 0.5 * acc_ref[...]
        if final:
            y = _rms(y, gf_ref[...])
        o_ref[...] = y


def _ffn(x, g, w_up, w_down, g_final=None, *, tm, tf=256):
    m, d = x.shape
    f = w_down.shape[0]
    nf = f // tf
    final = g_final is not None
    gf = g_final if final else g
    vmem = (4 * tm * d * 4 + tm * d * 2 + tm * d * 4 + 2 * 3 * d * tf * 2 + 4 * tm * tf * 4) * 5 // 4
    return pl.pallas_call(
        functools.partial(_ffn_body, nf=nf, final=final),
        out_shape=jax.ShapeDtypeStruct((m, d), F32),
        grid=(m // tm, nf),
        in_specs=[
            pl.BlockSpec((tm, d), lambda i, j: (i, 0)),
            pl.BlockSpec((1, d), lambda i, j: (0, 0)),
            pl.BlockSpec((d, tf), lambda i, j: (0, j)),
            pl.BlockSpec((d, tf), lambda i, j, nf=nf: (0, nf + j)),
            pl.BlockSpec((tf, d), lambda i, j: (j, 0)),
            pl.BlockSpec((1, d), lambda i, j: (0, 0)),
        ],
        out_specs=pl.BlockSpec((tm, d), lambda i, j: (i, 0)),
        scratch_shapes=[pltpu.VMEM((tm, d), BF16), pltpu.VMEM((tm, d), F32)],
        compiler_params=_params(("parallel", "arbitrary"), vmem),
        name="half_swiglu",
    )(x, g.reshape(1, d), w_up, w_up, w_down, gf.reshape(1, d))


def _rec_front(x, g, w_in, b_in):
    z = _dot(_rms(x, g).astype(BF16), w_in) + b_in
    return jax.nn.gelu(z[:, :D_RNN], approximate=True), z[:, D_RNN:]


def _rec_gates(xc, wa_ref, ba, wx_ref, bx, lam):
    ga, gx = [], []
    for blk in range(N_LRU_BLOCKS):
        xb = xc[:, blk * LRU_BLOCK:(blk + 1) * LRU_BLOCK].astype(BF16)
        ga.append(_dot(xb, wa_ref[blk]))
        gx.append(_dot(xb, wx_ref[blk]))
    gate_a = jax.nn.sigmoid(jnp.concatenate(ga, axis=1) + ba)
    gate_x = jax.nn.sigmoid(jnp.concatenate(gx, axis=1) + bx)
    log_a = -LRU_C * gate_a * jax.nn.softplus(-lam)
    a = jnp.exp(log_a)
    th = jnp.tanh(log_a)
    bterm = jnp.sqrt(-2.0 * th / (1.0 - th)) * gate_x * xc
    return a, bterm


def _rec_prompt_body(x_ref, g_ref, win_ref, bin_ref, wc_ref, bc_ref, wa_ref, ba_ref, wx_ref, bx_ref,
                     lam_ref, wout_ref, bout_ref, y_ref, hl_ref, cv_ref,
                     ext_ref, a_ref, b_ref, hs_ref, hc_ref, *, tm, nt):
    j = pl.program_id(1)
    head = SUBLANES

    @pl.when(j == 0)
    def _():
        hc_ref[...] = jnp.zeros_like(hc_ref)
        ext_ref[0:head, :] = jnp.zeros((head, D_RNN), F32)

    x = x_ref[...]
    yb, xb = _rec_front(x, g_ref[...], win_ref[...], bin_ref[...])
    ext_ref[head:head + tm, :] = xb
    wc = wc_ref[...]
    xc = bc_ref[...] + xb * wc[CONV_W - 1:CONV_W, :]
    for k in range(CONV_W - 1):
        s = head - (CONV_W - 1) + k
        xc = xc + ext_ref[s:s + tm, :] * wc[k:k + 1, :]
    tail = ext_ref[tm + head - (CONV_W - 1):tm + head, :]
    ext_ref[head - (CONV_W - 1):head, :] = tail

    a, bterm = _rec_gates(xc, wa_ref, ba_ref[...], wx_ref, bx_ref[...], lam_ref[...])
    a_ref[...] = a
    b_ref[...] = bterm

    row = lax.broadcasted_iota(jnp.int32, (SUBLANES, D_RNN), 0)

    def step(i, h):
        r0 = pl.multiple_of(i * SUBLANES, SUBLANES)
        aa = a_ref[pl.ds(r0, SUBLANES), :]
        bb = b_ref[pl.ds(r0, SUBLANES), :]
        for sh in (1, 2, 4):
            keep = row >= sh
            bb = jnp.where(keep, aa * pltpu.roll(bb, sh, axis=0) + bb, bb)
            aa = jnp.where(keep, aa * pltpu.roll(aa, sh, axis=0), aa)
        hb = aa * h + bb
        hs_ref[pl.ds(r0, SUBLANES), :] = hb
        return hb[SUBLANES - 1:SUBLANES, :]

    h_last = lax.fori_loop(0, tm // SUBLANES, step, hc_ref[...])
    hc_ref[...] = h_last

    out = _dot((hs_ref[...] * yb).astype(BF16), wout_ref[...]) + bout_ref[...]
    y_ref[...] = x + out

    @pl.when(j == nt - 1)
    def _():
        hl_ref[...] = h_last
        cv_ref[...] = tail


def _rec_weight_specs(imap):
    return [
        pl.BlockSpec((1, D_MODEL), imap(2)),
        pl.BlockSpec((D_MODEL, 2 * D_RNN), imap(2)),
        pl.BlockSpec((1, 2 * D_RNN), imap(2)),
        pl.BlockSpec((CONV_W, D_RNN), imap(2)),
        pl.BlockSpec((1, D_RNN), imap(2)),
        pl.BlockSpec((N_LRU_BLOCKS, LRU_BLOCK, LRU_BLOCK), imap(3)),
        pl.BlockSpec((1, D_RNN), imap(2)),
        pl.BlockSpec((N_LRU_BLOCKS, LRU_BLOCK, LRU_BLOCK), imap(3)),
        pl.BlockSpec((1, D_RNN), imap(2)),
        pl.BlockSpec((1, D_RNN), imap(2)),
        pl.BlockSpec((D_RNN, D_MODEL), imap(2)),
        pl.BlockSpec((1, D_MODEL), imap(2)),
    ]


def _rec_weights(g, prm):
    w_in, b_in, w_conv, b_conv, w_a, b_a, w_x, b_x, lam, w_out, b_out = prm
    r = lambda v: v.reshape(1, -1)
    return (r(g), w_in, r(b_in), w_conv, r(b_conv), w_a, r(b_a), w_x, r(b_x), r(lam), w_out, r(b_out))


def _rec_prompt(x, g, prm, *, tm):
    b, t, d = x.shape
    nt = t // tm
    imap = lambda n: (lambda bi, j: (0,) * n)
    vmem = (4 * tm * d * 4 + 2 * (3 * d * d * 2) + 10 * tm * d * 4) * 5 // 4
    y, hl, cv = pl.pallas_call(
        functools.partial(_rec_prompt_body, tm=tm, nt=nt),
        out_shape=(jax.ShapeDtypeStruct((b, t, d), F32),
                   jax.ShapeDtypeStruct((b, 1, D_RNN), F32),
                   jax.ShapeDtypeStruct((b, CONV_W - 1, D_RNN), F32)),
        grid=(b, nt),
        in_specs=[pl.BlockSpec((None, tm, d), lambda bi, j: (bi, j, 0))] + _rec_weight_specs(imap),
        out_specs=(pl.BlockSpec((None, tm, d), lambda bi, j: (bi, j, 0)),
                   pl.BlockSpec((None, 1, D_RNN), lambda bi, j: (bi, 0, 0)),
                   pl.BlockSpec((None, CONV_W - 1, D_RNN), lambda bi, j: (bi, 0, 0))),
        scratch_shapes=[pltpu.VMEM((tm + SUBLANES, D_RNN), F32), pltpu.VMEM((tm, D_RNN), F32),
                        pltpu.VMEM((tm, D_RNN), F32), pltpu.VMEM((tm, D_RNN), F32),
                        pltpu.VMEM((1, D_RNN), F32)],
        compiler_params=_params(("parallel", "arbitrary"), vmem),
        name="rec_mixer_prompt",
    )(x, *_rec_weights(g, prm))
    return y, hl[:, 0], cv


def _rec_sample_body(x_ref, h0_ref, c0_ref, g_ref, win_ref, bin_ref, wc_ref, bc_ref, wa_ref, ba_ref,
                     wx_ref, bx_ref, lam_ref, wout_ref, bout_ref, y_ref, hl_ref, cv_ref, ext_ref, hs_ref,
                     *, nb, nt):
    n = nb * nt
    head = (CONV_W - 1) * nb
    x = x_ref[...]
    yb, xb = _rec_front(x, g_ref[...], win_ref[...], bin_ref[...])
    ext_ref[0:head, :] = c0_ref[...]
    ext_ref[head:head + n, :] = xb
    wc = wc_ref[...]
    xc = bc_ref[...] + xb * wc[CONV_W - 1:CONV_W, :]
    for k in range(CONV_W - 1):
        xc = xc + ext_ref[k * nb:k * nb + n, :] * wc[k:k + 1, :]
    cv_ref[...] = ext_ref[n:n + head, :]

    a, bterm = _rec_gates(xc, wa_ref, ba_ref[...], wx_ref, bx_ref[...], lam_ref[...])
    h = h0_ref[...]
    for t in range(nt):
        h = a[t * nb:(t + 1) * nb, :] * h + bterm[t * nb:(t + 1) * nb, :]
        hs_ref[t * nb:(t + 1) * nb, :] = h
    hl_ref[...] = h
    out = _dot((hs_ref[...] * yb).astype(BF16), wout_ref[...]) + bout_ref[...]
    y_ref[...] = x + out


def _rec_sample(x, h0, c0, g, prm, *, nb, nt):
    n, d = x.shape
    imap = lambda k: (lambda i: (0,) * k)
    full = lambda shape: pl.BlockSpec(shape, imap(len(shape)))
    vmem = (2 * (3 * d * d * 2) + 40 * n * d * 4) * 5 // 4
    return pl.pallas_call(
        functools.partial(_rec_sample_body, nb=nb, nt=nt),
        out_shape=(jax.ShapeDtypeStruct((n, d), F32),
                   jax.ShapeDtypeStruct((nb, D_RNN), F32),
                   jax.ShapeDtypeStruct(((CONV_W - 1) * nb, D_RNN), F32)),
        grid=(1,),
        in_specs=[full((n, d)), full((nb, D_RNN)), full(((CONV_W - 1) * nb, D_RNN))] + _rec_weight_specs(imap),
        out_specs=(full((n, d)), full((nb, D_RNN)), full(((CONV_W - 1) * nb, D_RNN))),
        scratch_shapes=[pltpu.VMEM((n + (CONV_W - 1) * nb, D_RNN), F32), pltpu.VMEM((n, D_RNN), F32)],
        compiler_params=_params(("arbitrary",), vmem),
        name="rec_mixer_sample",
    )(x, h0, c0, *_rec_weights(g, prm))


def _rope(v, cs, sn):
    lane = lax.broadcasted_iota(jnp.int32, (v.shape[0], LANES), 1)
    first_half = (lane % HEAD_DIM) < (HEAD_DIM // 2)
    outs = []
    for c in range(v.shape[1] // LANES):
        x = v[:, c * LANES:(c + 1) * LANES]
        partner = jnp.where(first_half,
                            pltpu.roll(x, LANES - HEAD_DIM // 2, axis=1),
                            pltpu.roll(x, HEAD_DIM // 2, axis=1))
        outs.append(x * cs + partner * sn)
    return jnp.concatenate(outs, axis=1)


def _qkv_body(x_ref, g_ref, w_ref, cs_ref, sn_ref, o_ref, xn_ref):
    @pl.when(pl.program_id(1) == 0)
    def _():
        xn_ref[...] = _rms(x_ref[...], g_ref[...]).astype(BF16)

    z = _dot(xn_ref[...], w_ref[...])
    cs, sn = cs_ref[...], sn_ref[...]
    hw = N_HEADS * HEAD_DIM
    o_ref[:, 0:hw] = _rope(z[:, 0:hw], cs, sn)
    o_ref[:, hw:2 * hw] = _rope(z[:, hw:2 * hw], cs, sn)
    o_ref[:, 2 * hw:3 * hw] = z[:, 2 * hw:3 * hw]


def _qkv(x, g, w_qkv, cs, sn, *, tm):
    m, d = x.shape
    gw = 3 * N_HEADS * HEAD_DIM
    vmem = (2 * tm * d * 4 + tm * d * 2 + 2 * d * gw * 2 + 3 * tm * gw * 4 + 4 * tm * LANES * 4) * 5 // 4
    return pl.pallas_call(
        _qkv_body,
        out_shape=jax.ShapeDtypeStruct((m, N_GROUPS * gw), F32),
        grid=(m // tm, N_GROUPS),
        in_specs=[
            pl.BlockSpec((tm, d), lambda i, c: (i, 0)),
            pl.BlockSpec((1, d), lambda i, c: (0, 0)),
            pl.BlockSpec((d, gw), lambda i, c: (0, c)),
            pl.BlockSpec((tm, LANES), lambda i, c: (i, 0)),
            pl.BlockSpec((tm, LANES), lambda i, c: (i, 0)),
        ],
        out_specs=pl.BlockSpec((tm, gw), lambda i, c: (i, c)),
        scratch_shapes=[pltpu.VMEM((tm, d), BF16)],
        compiler_params=_params(("parallel", "arbitrary"), vmem),
        name="qkv_rope",
    )(x, g.reshape(1, d), w_qkv, cs, sn)


def _rope_tables(pos):
    half = HEAD_DIM // 2
    inv = jnp.power(ROPE_THETA, -jnp.arange(half, dtype=F32) / half)
    ang = pos.astype(F32)[:, None] * inv[None, :]
    cos, sin = jnp.cos(ang), jnp.sin(ang)
    reps = LANES // HEAD_DIM
    return (jnp.tile(jnp.concatenate([cos, cos], axis=1), (1, reps)),
            jnp.tile(jnp.concatenate([-sin, sin], axis=1), (1, reps)))


def _band_body(q_ref, k_ref, v_ref, o_ref, l_ref, kk_ref, vv_ref):
    j = pl.program_id(2)
    hw = N_HEADS * HEAD_DIM

    @pl.when(j == 0)
    def _():
        kk_ref[0:BAND, :] = jnp.zeros((BAND, hw), BF16)
        vv_ref[0:BAND, :] = jnp.zeros((BAND, hw), BF16)

    @pl.when(j > 0)
    def _():
        kk_ref[0:BAND, :] = kk_ref[BAND:2 * BAND, :]
        vv_ref[0:BAND, :] = vv_ref[BAND:2 * BAND, :]

    kk_ref[BAND:2 * BAND, :] = k_ref[...].astype(BF16)
    vv_ref[BAND:2 * BAND, :] = v_ref[...].astype(BF16)

    iq = lax.broadcasted_iota(jnp.int32, (BAND, 2 * BAND), 0)
    ik = lax.broadcasted_iota(jnp.int32, (BAND, 2 * BAND), 1)
    first = jnp.where(j > 0, 0, BAND)
    mask = ((ik < BAND) & (ik >= iq + first)) | ((ik >= BAND) & ((ik - BAND) <= iq))
    lane = lax.broadcasted_iota(jnp.int32, (BAND, LANES), 1)
    lo_head = lane < HEAD_DIM
    scale = HEAD_DIM ** -0.5

    for c in range(hw // LANES):
        sl = slice(c * LANES, (c + 1) * LANES)
        q2 = q_ref[:, sl] * scale
        k2 = kk_ref[:, sl]
        v2 = vv_ref[:, sl]
        res = []
        for qh in (jnp.where(lo_head, q2, 0.0), jnp.where(lo_head, 0.0, q2)):
            s = lax.dot_general(qh.astype(BF16), k2, (((1,), (1,)), ((), ())), preferred_element_type=F32)
            s = jnp.where(mask, s, NEG)
            m = jnp.max(s, axis=1, keepdims=True)
            p = jnp.exp(s - m)
            l = jnp.sum(p, axis=1, keepdims=True)
            res.append((_dot(p.astype(BF16), v2) / l, m + jnp.log(l)))
        o_ref[:, sl] = jnp.where(lo_head, res[0][0], res[1][0])
        l_ref[:, sl] = jnp.where(lo_head, res[0][1], res[1][1])


def _band_attention(qkv, g, *, batch, seq):
    dil = DILATIONS[g]
    length = seq // dil
    hw = N_HEADS * HEAD_DIM
    cols = qkv.shape[1] // hw
    view = qkv.reshape(batch, length, dil * qkv.shape[1])
    spec = lambda part: pl.BlockSpec((None, BAND, hw),
                                     lambda b, r, j: (b, j, r * cols + g * 3 + part))
    out_spec = pl.BlockSpec((None, BAND, hw), lambda b, r, j: (b, j, r))
    vmem = (2 * 5 * BAND * hw * 4 + 2 * 2 * BAND * hw * 2 + 64 * BAND * 2 * BAND * 4) * 5 // 4
    o, lse = pl.pallas_call(
        _band_body,
        out_shape=(jax.ShapeDtypeStruct((batch, length, dil * hw), F32),
                   jax.ShapeDtypeStruct((batch, length, dil * hw), F32)),
        grid=(batch, dil, length // BAND),
        in_specs=[spec(0), spec(1), spec(2)],
        out_specs=(out_spec, out_spec),
        scratch_shapes=[pltpu.VMEM((2 * BAND, hw), BF16), pltpu.VMEM((2 * BAND, hw), BF16)],
        compiler_params=_params(("parallel", "parallel", "arbitrary"), vmem),
        name="band_attention_d%d" % dil,
    )(view, view, view)
    return o.reshape(batch * seq, hw), lse.reshape(batch * seq, hw)


def _merge_body(x_ref, o0_ref, o1_ref, o2_ref, l0_ref, l1_ref, l2_ref, w_ref, y_ref):
    l0, l1, l2 = l0_ref[...], l1_ref[...], l2_ref[...]
    m = jnp.maximum(jnp.maximum(l0, l1), l2)
    e0, e1, e2 = jnp.exp(l0 - m), jnp.exp(l1 - m), jnp.exp(l2 - m)
    inv = 1.0 / (e0 + e1 + e2)
    o = (e0 * inv) * o0_ref[...] + (e1 * inv) * o1_ref[...] + (e2 * inv) * o2_ref[...]
    y_ref[...] = x_ref[...] + _dot(o.astype(BF16), w_ref[...])


def _merge_project(x, outs, lses, w_o, *, tm):
    m, d = x.shape
    row = pl.BlockSpec((tm, d), lambda i: (i, 0))
    vmem = (2 * 8 * tm * d * 4 + 2 * d * d * 2 + 6 * tm * d * 4) * 5 // 4
    return pl.pallas_call(
        _merge_body,
        out_shape=jax.ShapeDtypeStruct((m, d), F32),
        grid=(m // tm,),
        in_specs=[row] * 7 + [pl.BlockSpec((d, d), lambda i: (0, 0))],
        out_specs=row,
        compiler_params=_params(("parallel",), vmem),
        name="merge_project",
    )(x, *outs, *lses, w_o)


def _sample_attn_body(q_ref, c0_ref, c1_ref, c2_ref, o_ref, *, nt):
    hw = N_HEADS * HEAD_DIM
    row_w = 2 * hw
    hp = LANES
    head_of_lane = lax.broadcasted_iota(jnp.int32, (hw, hp), 0) // HEAD_DIM
    sel = (head_of_lane == lax.broadcasted_iota(jnp.int32, (hw, hp), 1)).astype(BF16)
    head_of_lane_t = lax.broadcasted_iota(jnp.int32, (hp, hw), 1) // HEAD_DIM
    sel_t = (head_of_lane_t == lax.broadcasted_iota(jnp.int32, (hp, hw), 0)).astype(BF16)
    pos = lax.broadcasted_iota(jnp.int32, (BAND, hp), 0)
    scale = HEAD_DIM ** -0.5
    caches = (c0_ref, c1_ref, c2_ref)

    def per_head(v):
        return _dot_split(jnp.broadcast_to(v, (SUBLANES, hw)), sel)[0:1, :]

    def per_lane(v):
        return _dot_split(jnp.broadcast_to(v, (SUBLANES, hp)), sel_t)[0:1, :]

    for t in range(nt):
        outs, lses = [], []
        for g in range(N_GROUPS):
            dil = DILATIONS[g]
            base = g * 3 * hw
            q = q_ref[t, :, base:base + hw] * scale
            off = 0 if dil == 1 else t * row_w
            kc = caches[g][:, off:off + hw]
            vc = caches[g][:, off + hw:off + row_w]
            s = _dot_split(kc * q, sel)
            if dil == 1:
                s = jnp.where(pos >= t, s, NEG)
                new = list(range(t + 1))
            else:
                new = [t]
            s_new = [per_head(q_ref[u, :, base + hw:base + 2 * hw] * q) for u in new]
            m = jnp.max(s, axis=0, keepdims=True)
            for sn in s_new:
                m = jnp.maximum(m, sn)
            p = jnp.exp(s - m)
            l = jnp.sum(p, axis=0, keepdims=True)
            acc = jnp.sum(_dot_split(p, sel_t) * vc, axis=0, keepdims=True)
            for u, sn in zip(new, s_new):
                pn = jnp.exp(sn - m)
                l = l + pn
                acc = acc + per_lane(pn) * q_ref[u, :, base + 2 * hw:base + 3 * hw]
            outs.append(acc / per_lane(l))
            lses.append(m + jnp.log(l))
        m = jnp.maximum(jnp.maximum(lses[0], lses[1]), lses[2])
        es = [jnp.exp(ls - m) for ls in lses]
        inv = 1.0 / (es[0] + es[1] + es[2])
        merged = per_lane(es[0] * inv) * outs[0]
        for e, o in zip(es[1:], outs[1:]):
            merged = merged + per_lane(e * inv) * o
        o_ref[t:t + 1, :] = merged


def _sample_attention(qkv, caches, layer, *, nb, nt):
    hw = N_HEADS * HEAD_DIM
    q4 = qkv.reshape(nt, nb, 1, qkv.shape[1])
    views, specs = [], []
    for g, c in enumerate(caches):
        dil = DILATIONS[g]
        width = dil * 2 * hw
        views.append(c.reshape(c.shape[0] * nb, c.shape[2] // dil, width))
        blk = min(width, nt * 2 * hw)
        specs.append(pl.BlockSpec((None, BAND, blk), lambda b, layer=layer: (layer * nb + b, 0, 0)))
    vmem = (2 * (sum(s.block_shape[2] for s in specs) * BAND * 4 + nt * SUBLANES * qkv.shape[1] * 4)
            + 48 * BAND * hw * 4) * 5 // 4
    return pl.pallas_call(
        functools.partial(_sample_attn_body, nt=nt),
        out_shape=jax.ShapeDtypeStruct((nb, nt, hw), F32),
        grid=(nb,),
        in_specs=[pl.BlockSpec((nt, None, 1, qkv.shape[1]), lambda b: (0, b, 0, 0))] + specs,
        out_specs=pl.BlockSpec((None, nt, hw), lambda b: (b, 0, 0)),
        compiler_params=_params(("parallel",), vmem),
        name="sample_attention",
    )(q4, *views)


def _project_body(x_ref, a_ref, w_ref, y_ref):
    y_ref[...] = x_ref[...] + _dot(a_ref[...].astype(BF16), w_ref[...])


def _project_residual(x, a, w):
    m, d = x.shape
    full = lambda shape: pl.BlockSpec(shape, lambda i: (0,) * len(shape))
    return pl.pallas_call(
        _project_body,
        out_shape=jax.ShapeDtypeStruct((m, d), F32),
        grid=(1,),
        in_specs=[full((m, d)), full(a.shape), full(w.shape)],
        out_specs=full((m, d)),
        compiler_params=_params(("arbitrary",), 16 * 1024 * 1024),
        name="project_residual",
    )(x, a, w)


def _tile(m, pref):
    t = min(m, pref)
    while m % t:
        t //= 2
    return t


def kernel(x_prompt, x_sample, state_h, state_conv, cache_kv_w128, cache_kv_w512, cache_kv_w2048,
           norm_ffn1, norm_mix, norm_ffn2, w_ffn1_up, w_ffn1_down, w_ffn2_up, w_ffn2_down,
           w_rec_in, b_rec_in, w_conv, b_conv, w_gate_a, b_gate_a, w_gate_x, b_gate_x, lru_lambda,
           w_rec_out, b_rec_out, w_qkv, w_attn_out, norm_final):
    caches = (cache_kv_w128, cache_kv_w512, cache_kv_w2048)
    batch, seq, d = x_prompt.shape
    nb, nt, _ = x_sample.shape
    depth = norm_mix.shape[0]
    hw = N_HEADS * HEAD_DIM
    bf = lambda w: w.astype(BF16)

    xp = x_prompt.reshape(batch * seq, d)
    xs = x_sample.transpose(1, 0, 2).reshape(nt * nb, d)
    tm_p = _tile(batch * seq, 1024)
    tm_s = _tile(nt * nb, 1024)

    cs_p, sn_p = _rope_tables(jnp.arange(seq))
    cs_p, sn_p = jnp.tile(cs_p, (batch, 1)), jnp.tile(sn_p, (batch, 1))
    cs_s, sn_s = _rope_tables(PAST_LEN + jnp.arange(nt))
    cs_s, sn_s = jnp.repeat(cs_s, nb, axis=0), jnp.repeat(sn_s, nb, axis=0)

    h_p, h_s, conv_p, conv_s = [], [], [], []
    kv_p = [[] for _ in range(N_GROUPS)]
    kv_s = [[] for _ in range(N_GROUPS)]
    for i in range(depth):
        up1, down1 = bf(w_ffn1_up[i]), bf(w_ffn1_down[i])
        xp = _ffn(xp, norm_ffn1[i], up1, down1, tm=tm_p)
        xs = _ffn(xs, norm_ffn1[i], up1, down1, tm=tm_s)
        j = i // 2
        if i % 2 == 0:
            prm = (bf(w_rec_in[j]), b_rec_in[j], w_conv[j], b_conv[j], bf(w_gate_a[j]), b_gate_a[j],
                   bf(w_gate_x[j]), b_gate_x[j], lru_lambda[j], bf(w_rec_out[j]), b_rec_out[j])
            xp3, hl_p, cv_p = _rec_prompt(xp.reshape(batch, seq, d), norm_mix[i], prm, tm=_tile(seq, 512))
            xp = xp3.reshape(batch * seq, d)
            c0 = state_conv[j].transpose(1, 0, 2).reshape((CONV_W - 1) * nb, D_RNN)
            xs, hl_s, cv_s = _rec_sample(xs, state_h[j], c0, norm_mix[i], prm, nb=nb, nt=nt)
            h_p.append(hl_p)
            h_s.append(hl_s)
            conv_p.append(cv_p)
            conv_s.append(cv_s.reshape(CONV_W - 1, nb, D_RNN).transpose(1, 0, 2))
        else:
            wq, wo = bf(w_qkv[j]), bf(w_attn_out[j])
            qkv_p = _qkv(xp, norm_mix[i], wq, cs_p, sn_p, tm=_tile(batch * seq, 512))
            outs, lses = zip(*[_band_attention(qkv_p, g, batch=batch, seq=seq) for g in range(N_GROUPS)])
            xp = _merge_project(xp, outs, lses, wo, tm=_tile(batch * seq, 512))
            qkv_s = _qkv(xs, norm_mix[i], wq, cs_s, sn_s, tm=tm_s)
            heads_s = _sample_attention(qkv_s, caches, j, nb=nb, nt=nt)
            xs = _project_residual(xs, heads_s.transpose(1, 0, 2).reshape(nt * nb, hw), wo)
            qp = qkv_p.reshape(batch, seq, N_GROUPS, 3, N_HEADS, HEAD_DIM)
            qs = qkv_s.reshape(nt, nb, N_GROUPS, 3, N_HEADS, HEAD_DIM)
            for g in range(N_GROUPS):
                c = min(WINDOWS[g], seq)
                kv_p[g].append(qp[:, seq - c:, g, 1:3])
                kv_s[g].append(qs[:, :, g, 1:3].transpose(1, 0, 2, 3, 4))
        last = i == depth - 1
        up2, down2 = bf(w_ffn2_up[i]), bf(w_ffn2_down[i])
        xp = _ffn(xp, norm_ffn2[i], up2, down2, norm_final if last else None, tm=tm_p)
        xs = _ffn(xs, norm_ffn2[i], up2, down2, norm_final if last else None, tm=tm_s)

    y_prompt = xp.reshape(batch, seq, d)
    y_sample = xs.reshape(nt, nb, d).transpose(1, 0, 2)
    return (y_prompt, y_sample,
            jnp.stack(h_p, 0), jnp.stack(h_s, 0), jnp.stack(conv_p, 0), jnp.stack(conv_s, 0),
            jnp.stack(kv_p[0], 0), jnp.stack(kv_s[0], 0),
            jnp.stack(kv_p[1], 0), jnp.stack(kv_s[1], 0),
            jnp.stack(kv_p[2], 0), jnp.stack(kv_s[2], 0))
```
